```python
import math
import jax, jax.numpy as jnp
from jax import lax
import numpy as np

D_MODEL = 1024
BATCH = 2
SEQ = 8192
DEPTH = 4

W_A = D_MODEL // 2
CONV_A_WIDTH = 31
DIFF_HEAD_DIM = 64
N_DIFF_HEADS = 4
W_B = N_DIFF_HEADS * 2 * DIFF_HEAD_DIM
W_C = D_MODEL // 2
CONV_C_WIDTH = 3
N_BRANCH = 3
Q_BLOCK = 128
NORM_EPS = 1e-6
SUBLN_EPS = 1e-5
SPLIT_SIZES = (W_A, W_A, W_A,
               W_B, W_B, W_B, W_B,
               W_C, W_C, W_C, W_C,
               N_BRANCH * D_MODEL)
N_IN = sum(SPLIT_SIZES)
SPLIT_IDX = tuple(int(i) for i in np.cumsum(SPLIT_SIZES)[:-1])

kernel_name = "hybrid_conformer_diffattn_shortconv_gated"


def rmsnorm(x, g, eps=NORM_EPS):
    xf = x.astype(jnp.float32)
    y = xf * lax.rsqrt(jnp.mean(xf * xf, axis=-1, keepdims=True) + eps)
    return (y * g.astype(jnp.float32)).astype(x.dtype)


def layernorm(x, g, b, eps=NORM_EPS):
    xf = x.astype(jnp.float32)
    mu = jnp.mean(xf, axis=-1, keepdims=True)
    var = jnp.mean(jnp.square(xf - mu), axis=-1, keepdims=True)
    y = (xf - mu) * lax.rsqrt(var + eps)
    return (y * g.astype(jnp.float32) + b.astype(jnp.float32)).astype(x.dtype)


def causal_depthwise_conv(u, w):
    k, c = w.shape
    return lax.conv_general_dilated(
        u, w[:, None, :].astype(u.dtype), window_strides=(1,), padding=[(k - 1, 0)],
        dimension_numbers=("NWC", "WIO", "NWC"), feature_group_count=c)


def diff_attention(q, k, v, lam):
    b, s, h, _, dh = q.shape
    nb = s // Q_BLOCK
    qf = q.astype(jnp.float32) * (dh ** -0.5)
    kf = k.astype(jnp.float32)
    vf = v.astype(jnp.float32)
    q_blocks = jnp.moveaxis(qf.reshape(b, nb, Q_BLOCK, h, 2, dh), 1, 0)
    starts = jnp.arange(nb, dtype=jnp.int32) * Q_BLOCK
    kpos = jnp.arange(s, dtype=jnp.int32)

    def one_block(args):
        qb, start = args
        scores = jnp.einsum("bqhmd,bkhmd->bhmqk", qb, kf)
        qpos = start + jnp.arange(Q_BLOCK, dtype=jnp.int32)
        causal = kpos[None, :] <= qpos[:, None]
        scores = jnp.where(causal, scores, -jnp.inf)
        p = jax.nn.softmax(scores, axis=-1)
        a = p[:, :, 0] - lam * p[:, :, 1]
        return jnp.einsum("bhqk,bkhe->bqhe", a, vf)

    o = lax.map(one_block, (q_blocks, starts))
    return jnp.moveaxis(o, 0, 1).reshape(b, s, h, 2 * dh)


def setup_inputs(seed: int = 0) -> dict:
    key = jax.random.key(seed)
    ks = jax.random.split(key, 20)
    f32 = jnp.float32
    nrm = lambda k, shape, scale: jax.random.normal(k, shape, f32) * scale
    return {
        "x": jax.random.normal(ks[0], (BATCH, SEQ, D_MODEL), f32),
        "norm_g": 1.0 + nrm(ks[1], (DEPTH, D_MODEL), 0.05),
        "w_in": nrm(ks[2], (DEPTH, D_MODEL, N_IN), D_MODEL ** -0.5),
        "gate_b": nrm(ks[3], (DEPTH, N_BRANCH * D_MODEL), 0.02),
        "conv_a_w": nrm(ks[4], (DEPTH, CONV_A_WIDTH, W_A), CONV_A_WIDTH ** -0.5),
        "conv_a_b": nrm(ks[5], (DEPTH, W_A), 0.02),
        "ln_a_g": 1.0 + nrm(ks[6], (DEPTH, W_A), 0.05),
        "ln_a_b": nrm(ks[7], (DEPTH, W_A), 0.02),
        "w_a_out": nrm(ks[8], (DEPTH, W_A, D_MODEL), W_A ** -0.5),
        "lam_q1": nrm(ks[9], (DEPTH, DIFF_HEAD_DIM), 0.1),
        "lam_k1": nrm(ks[10], (DEPTH, DIFF_HEAD_DIM), 0.1),
        "lam_q2": nrm(ks[11], (DEPTH, DIFF_HEAD_DIM), 0.1),
        "lam_k2": nrm(ks[12], (DEPTH, DIFF_HEAD_DIM), 0.1),
        "subln_g": 1.0 + nrm(ks[13], (DEPTH, 2 * DIFF_HEAD_DIM), 0.05),
        "w_b_out": nrm(ks[14], (DEPTH, W_B, D_MODEL), W_B ** -0.5),
        "conv_c_w": nrm(ks[15], (DEPTH, CONV_C_WIDTH, W_C), CONV_C_WIDTH ** -0.5),
        "w_c_out": nrm(ks[16], (DEPTH, W_C, D_MODEL), W_C ** -0.5),
        "w_o": nrm(ks[17], (DEPTH, D_MODEL, D_MODEL), D_MODEL ** -0.5),
        "final_g": 1.0 + nrm(ks[18], (D_MODEL,), 0.05),
    }


def reference(x, norm_g, w_in, gate_b, conv_a_w, conv_a_b, ln_a_g, ln_a_b, w_a_out,
              lam_q1, lam_k1, lam_q2, lam_k2, subln_g, w_b_out, conv_c_w, w_c_out,
              w_o, final_g):
    b, s, _ = x.shape
    for l in range(DEPTH):
        h = rmsnorm(x, norm_g[l])
        proj = h @ w_in[l].astype(h.dtype)
        (a_val, a_gate, a_z, q, k, v, b_z,
         c_b, c_c, c_x, c_z, gates) = jnp.split(proj, SPLIT_IDX, axis=-1)

        u = a_val * jax.nn.sigmoid(a_gate)
        u = causal_depthwise_conv(u, conv_a_w[l]) + conv_a_b[l].astype(u.dtype)
        u = jax.nn.silu(layernorm(u, ln_a_g[l], ln_a_b[l]))
        y_a = (u * jax.nn.silu(a_z)) @ w_a_out[l].astype(u.dtype)

        lam_init = 0.8 - 0.6 * math.exp(-0.3 * l)
        lam = (jnp.exp(jnp.sum(lam_q1[l].astype(jnp.float32) * lam_k1[l].astype(jnp.float32)))
               - jnp.exp(jnp.sum(lam_q2[l].astype(jnp.float32) * lam_k2[l].astype(jnp.float32)))
               + lam_init)
        qh = q.reshape(b, s, N_DIFF_HEADS, 2, DIFF_HEAD_DIM)
        kh = k.reshape(b, s, N_DIFF_HEADS, 2, DIFF_HEAD_DIM)
        vh = v.reshape(b, s, N_DIFF_HEADS, 2 * DIFF_HEAD_DIM)
        o = diff_attention(qh, kh, vh, lam)
        o = rmsnorm(o, subln_g[l], SUBLN_EPS) * (1.0 - lam_init)
        o = o.reshape(b, s, W_B).astype(x.dtype)
        y_b = (o * jax.nn.silu(b_z)) @ w_b_out[l].astype(o.dtype)

        yc = c_b * causal_depthwise_conv(c_c * c_x, conv_c_w[l])
        y_c = (yc * jax.nn.silu(c_z)) @ w_c_out[l].astype(yc.dtype)

        g = jax.nn.sigmoid(gates + gate_b[l].astype(gates.dtype)).reshape(b, s, N_BRANCH, D_MODEL)
        merged = g[:, :, 0] * y_a + g[:, :, 1] * y_b + g[:, :, 2] * y_c
        x = x + merged @ w_o[l].astype(merged.dtype)
    return rmsnorm(x, final_g)
```

```python
import functools
import math

import jax
import jax.numpy as jnp
from jax import lax
from jax.experimental import pallas as pl
from jax.experimental.pallas import tpu as pltpu

D_MODEL = 1024
DEPTH = 4
W_A = 512
CONV_A_WIDTH = 31
DIFF_HEAD_DIM = 64
N_DIFF_HEADS = 4
HEAD_W = 2 * DIFF_HEAD_DIM
W_B = N_DIFF_HEADS * HEAD_W
W_C = 512
CONV_C_WIDTH = 3
NORM_EPS = 1e-6
SUBLN_EPS = 1e-5
N_IN = 3 * W_A + 4 * W_B + 4 * W_C + 3 * D_MODEL

A_VAL, A_GATE, A_Z = 0, W_A, 2 * W_A
B_Q = 3 * W_A
B_K, B_V, B_Z = B_Q + W_B, B_Q + 2 * W_B, B_Q + 3 * W_B
C_B = B_Q + 4 * W_B
C_C, C_X, C_Z = C_B + W_C, C_B + 2 * W_C, C_B + 3 * W_C
G_0 = C_B + 4 * W_C
G_1, G_2 = G_0 + D_MODEL, G_0 + 2 * D_MODEL

LANES = 128
SUBLANES = 8
VMEM_LIMIT_BYTES = 56 * 1024 * 1024

TM_IN = 256
TM_OUT = 512
TQ = 512
TK = 512
CONV_ROWS = 64
HALO_A = 32
HALO_C = 8

F32 = jnp.float32
BF16 = jnp.bfloat16


def _sigmoid(x):
    return jax.nn.sigmoid(x)


def _silu(x):
    return x * jax.nn.sigmoid(x)


def _inproj_kernel(x_ref, ng_ref, win_ref, gb_ref, caw_ref, cab_ref, lag_ref, lab_ref, ccw_ref,
                   wao_ref, wco_ref,
                   q_ref, k_ref, v_ref, szb_ref, g1_ref, mac_ref,
                   exta_ref, extc_ref, conv_ref):
    tm = x_ref.shape[1]

    @pl.when(pl.program_id(1) == 0)
    def _():
        exta_ref[0:HALO_A, :] = jnp.zeros((HALO_A, W_A), F32)
        extc_ref[0:HALO_C, :] = jnp.zeros((HALO_C, W_C), F32)

    x = x_ref[0]
    ms = jnp.mean(x * x, axis=-1, keepdims=True)
    h = (x * lax.rsqrt(ms + NORM_EPS) * ng_ref[...]).astype(BF16)

    def proj(lo, hi):
        return jnp.dot(h, win_ref[:, lo:hi], preferred_element_type=F32)

    pa = proj(A_VAL, B_Q)
    exta_ref[HALO_A:HALO_A + tm, :] = pa[:, 0:W_A] * _sigmoid(pa[:, W_A:2 * W_A])
    base = HALO_A - (CONV_A_WIDTH - 1)
    for r0 in range(0, tm, CONV_ROWS):
        for c0 in range(0, W_A, LANES):
            acc = jnp.zeros((CONV_ROWS, LANES), F32)
            for kk in range(CONV_A_WIDTH):
                tap = exta_ref[r0 + base + kk:r0 + base + kk + CONV_ROWS, c0:c0 + LANES]
                acc = acc + caw_ref[kk:kk + 1, c0:c0 + LANES] * tap
            conv_ref[r0:r0 + CONV_ROWS, c0:c0 + LANES] = acc
    exta_ref[0:HALO_A, :] = exta_ref[tm:tm + HALO_A, :]
    ca = conv_ref[...] + cab_ref[...]
    mu = jnp.mean(ca, axis=-1, keepdims=True)
    cen = ca - mu
    var = jnp.mean(cen * cen, axis=-1, keepdims=True)
    ua = _silu(cen * lax.rsqrt(var + NORM_EPS) * lag_ref[...] + lab_ref[...])
    za = (ua * _silu(pa[:, 2 * W_A:3 * W_A])).astype(BF16)
    ya = jnp.dot(za, wao_ref[...], preferred_element_type=F32)
    g0 = _sigmoid(proj(G_0, G_1) + gb_ref[:, 0:D_MODEL])
    mac = g0 * ya

    pc = proj(C_B, G_0)
    extc_ref[HALO_C:HALO_C + tm, :] = pc[:, W_C:2 * W_C] * pc[:, 2 * W_C:3 * W_C]
    cbase = HALO_C - (CONV_C_WIDTH - 1)
    cc = ccw_ref[0:1, :] * extc_ref[cbase:cbase + tm, :]
    for kk in range(1, CONV_C_WIDTH):
        cc = cc + ccw_ref[kk:kk + 1, :] * extc_ref[cbase + kk:cbase + kk + tm, :]
    extc_ref[0:HALO_C, :] = extc_ref[tm:tm + HALO_C, :]
    zc = (pc[:, 0:W_C] * cc * _silu(pc[:, 3 * W_C:4 * W_C])).astype(BF16)
    yc = jnp.dot(zc, wco_ref[...], preferred_element_type=F32)
    g2 = _sigmoid(proj(G_2, N_IN) + gb_ref[:, 2 * D_MODEL:3 * D_MODEL])
    mac_ref[0] = mac + g2 * yc

    g1_ref[0] = _sigmoid(proj(G_1, G_2) + gb_ref[:, D_MODEL:2 * D_MODEL])

    pb = proj(B_Q, C_B)
    q_ref[0] = (pb[:, 0:W_B] * (DIFF_HEAD_DIM ** -0.5)).astype(BF16)
    k_ref[0] = pb[:, W_B:2 * W_B].astype(BF16)
    v_ref[0] = pb[:, 2 * W_B:3 * W_B].astype(BF16)
    szb_ref[0] = _silu(pb[:, 3 * W_B:4 * W_B])


def _inproj(l, x, ng, win, gb, caw, cab, lag, lab, ccw, wao, wco):
    b, s, _ = x.shape
    tm = TM_IN
    row = lambda bi, ti: (bi, ti, 0)
    lay3 = lambda bi, ti: (l, 0, 0)
    vec = lambda w: pl.BlockSpec((None, 1, w), lay3)
    one = pl.Buffered(1)
    return pl.pallas_call(
        _inproj_kernel,
        grid=(b, s // tm),
        in_specs=[
            pl.BlockSpec((1, tm, D_MODEL), row),
            vec(D_MODEL),
            pl.BlockSpec((None, D_MODEL, N_IN), lay3, pipeline_mode=one),
            vec(3 * D_MODEL),
            pl.BlockSpec((None, CONV_A_WIDTH, W_A), lay3),
            vec(W_A), vec(W_A), vec(W_A),
            pl.BlockSpec((None, CONV_C_WIDTH, W_C), lay3),
            pl.BlockSpec((None, W_A, D_MODEL), lay3, pipeline_mode=one),
            pl.BlockSpec((None, W_C, D_MODEL), lay3, pipeline_mode=one),
        ],
        out_specs=[
            pl.BlockSpec((1, tm, W_B), row),
            pl.BlockSpec((1, tm, W_B), row),
            pl.BlockSpec((1, tm, W_B), row),
            pl.BlockSpec((1, tm, W_B), row),
            pl.BlockSpec((1, tm, D_MODEL), row),
            pl.BlockSpec((1, tm, D_MODEL), row),
        ],
        out_shape=[
            jax.ShapeDtypeStruct((b, s, W_B), BF16),
            jax.ShapeDtypeStruct((b, s, W_B), BF16),
            jax.ShapeDtypeStruct((b, s, W_B), BF16),
            jax.ShapeDtypeStruct((b, s, W_B), F32),
            jax.ShapeDtypeStruct((b, s, D_MODEL), F32),
            jax.ShapeDtypeStruct((b, s, D_MODEL), F32),
        ],
        scratch_shapes=[
            pltpu.VMEM((HALO_A + tm, W_A), F32),
            pltpu.VMEM((HALO_C + tm, W_C), F32),
            pltpu.VMEM((tm, W_A), F32),
        ],
        compiler_params=pltpu.CompilerParams(
            dimension_semantics=("arbitrary", "arbitrary"),
            vmem_limit_bytes=VMEM_LIMIT_BYTES),
        name=f"inproj_l{l}",
    )(x, ng, win, gb, caw, cab, lag, lab, ccw, wao, wco)


def _attn_kernel(lam_ref, sg_ref, q_ref, k_ref, v_ref, szb_ref, o_ref,
                 m1_ref, l1_ref, a1_ref, m2_ref, l2_ref, a2_ref, *, lam_init):
    tq = q_ref.shape[1]
    qi = pl.program_id(2)

    q = q_ref[0]
    lane = lax.broadcasted_iota(jnp.int32, (1, HEAD_W), 1)
    zero = jnp.zeros_like(q)
    q_maps = (jnp.where(lane < DIFF_HEAD_DIM, q, zero), jnp.where(lane >= DIFF_HEAD_DIM, q, zero))
    streams = ((m1_ref, l1_ref, a1_ref), (m2_ref, l2_ref, a2_ref))
    contract_last = (((1,), (1,)), ((), ()))

    kd = k_ref[0, pl.ds(pl.multiple_of(qi * tq, tq), tq), :]
    vd = v_ref[0, pl.ds(pl.multiple_of(qi * tq, tq), tq), :]
    rows = lax.broadcasted_iota(jnp.int32, (tq, tq), 0)
    cols = lax.broadcasted_iota(jnp.int32, (tq, tq), 1)
    causal = cols <= rows
    for qm, (m_ref, l_ref, a_ref) in zip(q_maps, streams):
        s = lax.dot_general(qm, kd, contract_last, preferred_element_type=F32)
        s = jnp.where(causal, s, -jnp.inf)
        m = jnp.max(s, axis=-1, keepdims=True)
        p = jnp.exp(s - m)
        m_ref[...] = m
        l_ref[...] = jnp.sum(p, axis=-1, keepdims=True)
        a_ref[...] = jnp.dot(p.astype(BF16), vd, preferred_element_type=F32)

    def body(j, carry):
        off = pl.multiple_of(j * TK, TK)
        kt = k_ref[0, pl.ds(off, TK), :]
        vt = v_ref[0, pl.ds(off, TK), :]
        for qm, (m_ref, l_ref, a_ref) in zip(q_maps, streams):
            s = lax.dot_general(qm, kt, contract_last, preferred_element_type=F32)
            m_old = m_ref[...]
            m_new = jnp.maximum(m_old, jnp.max(s, axis=-1, keepdims=True))
            alpha = jnp.exp(m_old - m_new)
            p = jnp.exp(s - m_new)
            m_ref[...] = m_new
            l_ref[...] = alpha * l_ref[...] + jnp.sum(p, axis=-1, keepdims=True)
            a_ref[...] = alpha * a_ref[...] + jnp.dot(p.astype(BF16), vt, preferred_element_type=F32)
        return carry

    lax.fori_loop(0, qi * (tq // TK), body, 0)

    lv = lam_ref[...]
    lam = (jnp.exp(jnp.sum(lv[0:1, :] * lv[1:2, :], axis=-1, keepdims=True))
           - jnp.exp(jnp.sum(lv[2:3, :] * lv[3:4, :], axis=-1, keepdims=True)) + lam_init)
    o = a1_ref[...] / l1_ref[...] - lam * (a2_ref[...] / l2_ref[...])
    ms = jnp.mean(o * o, axis=-1, keepdims=True)
    o = o * lax.rsqrt(ms + SUBLN_EPS) * sg_ref[...] * (1.0 - lam_init)
    o_ref[0] = (o * szb_ref[0]).astype(BF16)


def _attn(l, lam_vecs, sg, q, k, v, szb):
    b, s, _ = q.shape
    lam_init = 0.8 - 0.6 * math.exp(-0.3 * l)
    qblk = lambda bi, hi, qi: (bi, qi, hi)
    kvblk = lambda bi, hi, qi: (bi, 0, hi)
    stat = lambda: pltpu.VMEM((TQ, 1), F32)
    accs = lambda: pltpu.VMEM((TQ, HEAD_W), F32)
    return pl.pallas_call(
        functools.partial(_attn_kernel, lam_init=lam_init),
        grid=(b, N_DIFF_HEADS, s // TQ),
        in_specs=[
            pl.BlockSpec((None, 4, DIFF_HEAD_DIM), lambda bi, hi, qi: (l, 0, 0)),
            pl.BlockSpec((None, 1, HEAD_W), lambda bi, hi, qi: (l, 0, 0)),
            pl.BlockSpec((1, TQ, HEAD_W), qblk),
            pl.BlockSpec((1, s, HEAD_W), kvblk),
            pl.BlockSpec((1, s, HEAD_W), kvblk),
            pl.BlockSpec((1, TQ, HEAD_W), qblk),
        ],
        out_specs=pl.BlockSpec((1, TQ, HEAD_W), qblk),
        out_shape=jax.ShapeDtypeStruct((b, s, W_B), BF16),
        scratch_shapes=[stat(), stat(), accs(), stat(), stat(), accs()],
        compiler_params=pltpu.CompilerParams(
            dimension_semantics=("arbitrary", "arbitrary", "arbitrary"),
            vmem_limit_bytes=VMEM_LIMIT_BYTES),
        name=f"attn_l{l}",
    )(lam_vecs, sg, q, k, v, szb)


def _outproj_kernel(x_ref, ob_ref, g1_ref, mac_ref, wbo_ref, wo_ref, fg_ref, o_ref, *, final):
    yb = jnp.dot(ob_ref[0], wbo_ref[...], preferred_element_type=F32)
    merged = (mac_ref[0] + g1_ref[0] * yb).astype(BF16)
    xn = x_ref[0] + jnp.dot(merged, wo_ref[...], preferred_element_type=F32)
    if final:
        ms = jnp.mean(xn * xn, axis=-1, keepdims=True)
        xn = xn * lax.rsqrt(ms + NORM_EPS) * fg_ref[...]
    o_ref[0] = xn


def _outproj(l, x, ob, g1, mac, wbo, wo, fg):
    b, s, _ = x.shape
    tm = TM_OUT
    row = lambda bi, ti: (bi, ti, 0)
    lay3 = lambda bi, ti: (l, 0, 0)
    one = pl.Buffered(1)
    return pl.pallas_call(
        functools.partial(_outproj_kernel, final=(l == DEPTH - 1)),
        grid=(b, s // tm),
        in_specs=[
            pl.BlockSpec((1, tm, D_MODEL), row),
            pl.BlockSpec((1, tm, W_B), row),
            pl.BlockSpec((1, tm, D_MODEL), row),
            pl.BlockSpec((1, tm, D_MODEL), row),
            pl.BlockSpec((None, W_B, D_MODEL), lay3, pipeline_mode=one),
            pl.BlockSpec((None, D_MODEL, D_MODEL), lay3, pipeline_mode=one),
            pl.BlockSpec((1, D_MODEL), lambda bi, ti: (0, 0)),
        ],
        out_specs=pl.BlockSpec((1, tm, D_MODEL), row),
        out_shape=jax.ShapeDtypeStruct((b, s, D_MODEL), F32),
        compiler_params=pltpu.CompilerParams(
            dimension_semantics=("arbitrary", "arbitrary"),
            vmem_limit_bytes=VMEM_LIMIT_BYTES),
        name=f"outproj_l{l}",
    )(x, ob, g1, mac, wbo, wo, fg)


def kernel(x, norm_g, w_in, gate_b, conv_a_w, conv_a_b, ln_a_g, ln_a_b, w_a_out, lam_q1, lam_k1,
           lam_q2, lam_k2, subln_g, w_b_out, conv_c_w, w_c_out, w_o, final_g):
    assert x.shape[1] % TM_IN == 0 and x.shape[1] % TM_OUT == 0 and x.shape[1] % TQ == 0
    assert TQ == TK and TM_IN % CONV_ROWS == 0
    w_in_b = w_in.astype(BF16)
    w_a_out_b = w_a_out.astype(BF16)
    w_b_out_b = w_b_out.astype(BF16)
    w_c_out_b = w_c_out.astype(BF16)
    w_o_b = w_o.astype(BF16)
    lam_vecs = jnp.stack([lam_q1, lam_k1, lam_q2, lam_k2], axis=1)
    fg = final_g.reshape(1, D_MODEL)
    per_layer = lambda a: a.reshape(DEPTH, 1, a.shape[-1])
    ng, gb, cab, lag, lab, sg = map(per_layer, (norm_g, gate_b, conv_a_b, ln_a_g, ln_a_b, subln_g))
    for l in range(DEPTH):
        q, k, v, szb, g1, mac = _inproj(l, x, ng, w_in_b, gb, conv_a_w, cab, lag, lab,
                                        conv_c_w, w_a_out_b, w_c_out_b)
        ob = _attn(l, lam_vecs, sg, q, k, v, szb)
        x = _outproj(l, x, ob, g1, mac, w_b_out_b, w_o_b, fg)
    return x
```

```python
import functools
import math

import jax
import jax.numpy as jnp
from jax import lax
from jax.experimental import pallas as pl
from jax.experimental.pallas import tpu as pltpu

D_MODEL = 1024
DEPTH = 4
W_A = 512
CONV_A_WIDTH = 31
DIFF_HEAD_DIM = 64
N_DIFF_HEADS = 4
HEAD_W = 2 * DIFF_HEAD_DIM
W_B = N_DIFF_HEADS * HEAD_W
W_C = 512
CONV_C_WIDTH = 3
NORM_EPS = 1e-6
SUBLN_EPS = 1e-5
LOG2E = math.log2(math.e)
N_IN = 3 * W_A + 4 * W_B + 4 * W_C + 3 * D_MODEL

A_VAL, A_GATE, A_Z = 0, W_A, 2 * W_A
B_Q = 3 * W_A
B_K, B_V, B_Z = B_Q + W_B, B_Q + 2 * W_B, B_Q + 3 * W_B
C_B = B_Q + 4 * W_B
C_C, C_X, C_Z = C_B + W_C, C_B + 2 * W_C, C_B + 3 * W_C
G_0 = C_B + 4 * W_C
G_1, G_2 = G_0 + D_MODEL, G_0 + 2 * D_MODEL

LANES = 128
SUBLANES = 8
VMEM_LIMIT_BYTES = 56 * 1024 * 1024

TM_IN = 256
TM_OUT = 512
TQ = 512
TK = 256
CONV_ROWS = 64
HALO_A = 32
HALO_C = 8

F32 = jnp.float32
BF16 = jnp.bfloat16


def _sigmoid(x):
    return jax.nn.sigmoid(x)


def _silu(x):
    return x * jax.nn.sigmoid(x)


def _inproj_kernel(x_ref, ng_ref, win_ref, gb_ref, caw_ref, cab_ref, lag_ref, lab_ref, ccw_ref,
                   wao_ref, wco_ref,
                   q_ref, k_ref, v_ref, szb_ref, g1_ref, mac_ref,
                   exta_ref, extc_ref, conv_ref):
    tm = x_ref.shape[1]

    @pl.when(pl.program_id(1) == 0)
    def _():
        exta_ref[0:HALO_A, :] = jnp.zeros((HALO_A, W_A), F32)
        extc_ref[0:HALO_C, :] = jnp.zeros((HALO_C, W_C), F32)

    x = x_ref[0]
    ms = jnp.mean(x * x, axis=-1, keepdims=True)
    h = (x * lax.rsqrt(ms + NORM_EPS) * ng_ref[...]).astype(BF16)

    def proj(lo, hi):
        return jnp.dot(h, win_ref[:, lo:hi], preferred_element_type=F32)

    pa = proj(A_VAL, B_Q)
    exta_ref[HALO_A:HALO_A + tm, :] = pa[:, 0:W_A] * _sigmoid(pa[:, W_A:2 * W_A])
    base = HALO_A - (CONV_A_WIDTH - 1)
    for r0 in range(0, tm, CONV_ROWS):
        for c0 in range(0, W_A, LANES):
            acc = jnp.zeros((CONV_ROWS, LANES), F32)
            for kk in range(CONV_A_WIDTH):
                tap = exta_ref[r0 + base + kk:r0 + base + kk + CONV_ROWS, c0:c0 + LANES]
                acc = acc + caw_ref[kk:kk + 1, c0:c0 + LANES] * tap
            conv_ref[r0:r0 + CONV_ROWS, c0:c0 + LANES] = acc
    exta_ref[0:HALO_A, :] = exta_ref[tm:tm + HALO_A, :]
    ca = conv_ref[...] + cab_ref[...]
    mu = jnp.mean(ca, axis=-1, keepdims=True)
    cen = ca - mu
    var = jnp.mean(cen * cen, axis=-1, keepdims=True)
    ua = _silu(cen * lax.rsqrt(var + NORM_EPS) * lag_ref[...] + lab_ref[...])
    za = (ua * _silu(pa[:, 2 * W_A:3 * W_A])).astype(BF16)
    ya = jnp.dot(za, wao_ref[...], preferred_element_type=F32)
    g0 = _sigmoid(proj(G_0, G_1) + gb_ref[:, 0:D_MODEL])
    mac = g0 * ya

    pc = proj(C_B, G_0)
    extc_ref[HALO_C:HALO_C + tm, :] = pc[:, W_C:2 * W_C] * pc[:, 2 * W_C:3 * W_C]
    cbase = HALO_C - (CONV_C_WIDTH - 1)
    cc = ccw_ref[0:1, :] * extc_ref[cbase:cbase + tm, :]
    for kk in range(1, CONV_C_WIDTH):
        cc = cc + ccw_ref[kk:kk + 1, :] * extc_ref[cbase + kk:cbase + kk + tm, :]
    extc_ref[0:HALO_C, :] = extc_ref[tm:tm + HALO_C, :]
    zc = (pc[:, 0:W_C] * cc * _silu(pc[:, 3 * W_C:4 * W_C])).astype(BF16)
    yc = jnp.dot(zc, wco_ref[...], preferred_element_type=F32)
    g2 = _sigmoid(proj(G_2, N_IN) + gb_ref[:, 2 * D_MODEL:3 * D_MODEL])
    mac_ref[0] = mac + g2 * yc

    g1_ref[0] = _sigmoid(proj(G_1, G_2) + gb_ref[:, D_MODEL:2 * D_MODEL])

    pb = proj(B_Q, C_B)
    q_ref[0] = (pb[:, 0:W_B] * (DIFF_HEAD_DIM ** -0.5 * LOG2E)).astype(BF16)
    k_ref[0] = pb[:, W_B:2 * W_B].astype(BF16)
    v_ref[0] = pb[:, 2 * W_B:3 * W_B].astype(BF16)
    szb_ref[0] = _silu(pb[:, 3 * W_B:4 * W_B])


def _inproj(l, x, ng, win, gb, caw, cab, lag, lab, ccw, wao, wco):
    b, s, _ = x.shape
    tm = TM_IN
    row = lambda bi, ti: (bi, ti, 0)
    lay3 = lambda bi, ti: (l, 0, 0)
    vec = lambda w: pl.BlockSpec((None, 1, w), lay3)
    one = pl.Buffered(1)
    return pl.pallas_call(
        _inproj_kernel,
        grid=(b, s // tm),
        in_specs=[
            pl.BlockSpec((1, tm, D_MODEL), row),
            vec(D_MODEL),
            pl.BlockSpec((None, D_MODEL, N_IN), lay3, pipeline_mode=one),
            vec(3 * D_MODEL),
            pl.BlockSpec((None, CONV_A_WIDTH, W_A), lay3),
            vec(W_A), vec(W_A), vec(W_A),
            pl.BlockSpec((None, CONV_C_WIDTH, W_C), lay3),
            pl.BlockSpec((None, W_A, D_MODEL), lay3, pipeline_mode=one),
            pl.BlockSpec((None, W_C, D_MODEL), lay3, pipeline_mode=one),
        ],
        out_specs=[
            pl.BlockSpec((1, tm, W_B), row),
            pl.BlockSpec((1, tm, W_B), row),
            pl.BlockSpec((1, tm, W_B), row),
            pl.BlockSpec((1, tm, W_B), row),
            pl.BlockSpec((1, tm, D_MODEL), row),
            pl.BlockSpec((1, tm, D_MODEL), row),
        ],
        out_shape=[
            jax.ShapeDtypeStruct((b, s, W_B), BF16),
            jax.ShapeDtypeStruct((b, s, W_B), BF16),
            jax.ShapeDtypeStruct((b, s, W_B), BF16),
            jax.ShapeDtypeStruct((b, s, W_B), F32),
            jax.ShapeDtypeStruct((b, s, D_MODEL), F32),
            jax.ShapeDtypeStruct((b, s, D_MODEL), F32),
        ],
        scratch_shapes=[
            pltpu.VMEM((HALO_A + tm, W_A), F32),
            pltpu.VMEM((HALO_C + tm, W_C), F32),
            pltpu.VMEM((tm, W_A), F32),
        ],
        compiler_params=pltpu.CompilerParams(
            dimension_semantics=("arbitrary", "arbitrary"),
            vmem_limit_bytes=VMEM_LIMIT_BYTES),
        name=f"inproj_l{l}",
    )(x, ng, win, gb, caw, cab, lag, lab, ccw, wao, wco)


def _attn_kernel(lam_ref, sg_ref, q_ref, k_ref, v_ref, szb_ref, o_ref,
                 vt_ref, sa_ref, sb_ref, m_ref, l_ref, acc_ref, *, lam_init):
    tq = q_ref.shape[1]
    s_len = k_ref.shape[1]
    qi = pl.program_id(2)

    @pl.when(qi == 0)
    def _():
        for r0 in range(0, s_len, tq):
            vt_ref[:, r0:r0 + tq] = v_ref[0, r0:r0 + tq, :].astype(F32).T.astype(BF16)

    q = q_ref[0]
    lane = lax.broadcasted_iota(jnp.int32, (1, HEAD_W), 1)
    zero = jnp.zeros_like(q)
    q_maps = (jnp.where(lane < DIFF_HEAD_DIM, q, zero), jnp.where(lane >= DIFF_HEAD_DIM, q, zero))
    contract_last = (((1,), (1,)), ((), ()))

    m_ref[...] = jnp.full(m_ref.shape, -jnp.inf, F32)
    l_ref[...] = jnp.zeros(l_ref.shape, F32)
    acc_ref[...] = jnp.zeros(acc_ref.shape, F32)

    def scores(key_off, dst_ref):
        kt = k_ref[0, pl.ds(key_off, TK), :]
        for mp in range(2):
            dst_ref[mp] = lax.dot_general(kt, q_maps[mp], contract_last,
                                          preferred_element_type=F32)

    def consume(key_off, src_ref, mask=None):
        vt = vt_ref[:, pl.ds(key_off, TK)]
        for mp in range(2):
            st = src_ref[mp]
            if mask is not None:
                st = jnp.where(mask, st, -jnp.inf)
            m_old = m_ref[mp]
            m_new = jnp.maximum(m_old, jnp.max(st, axis=0, keepdims=True))
            alpha = jnp.exp2(m_old - m_new)
            pt = jnp.exp2(st - m_new)
            m_ref[mp] = m_new
            l_ref[mp] = alpha * l_ref[mp] + jnp.sum(pt, axis=0, keepdims=True)
            acc_ref[mp] = alpha * acc_ref[mp] + jnp.dot(
                vt, pt.astype(BF16), preferred_element_type=F32)

    scores(0, sa_ref)

    def pair(i, carry):
        base = pl.multiple_of(i * (2 * TK), 2 * TK)
        scores(base + TK, sb_ref)
        consume(base, sa_ref)
        scores(base + 2 * TK, sa_ref)
        consume(base + TK, sb_ref)
        return carry

    lax.fori_loop(0, qi, pair, 0)

    doff = pl.multiple_of(qi * tq, tq)
    scores(doff + TK, sb_ref)
    kpos = lax.broadcasted_iota(jnp.int32, (TK, tq), 0)
    qpos = lax.broadcasted_iota(jnp.int32, (TK, tq), 1)
    consume(doff, sa_ref, mask=kpos <= qpos)
    consume(doff + TK, sb_ref, mask=kpos + TK <= qpos)

    lv = lam_ref[...]
    lam = (jnp.exp(jnp.sum(lv[0:1, :] * lv[1:2, :], axis=-1, keepdims=True))
           - jnp.exp(jnp.sum(lv[2:3, :] * lv[3:4, :], axis=-1, keepdims=True)) + lam_init)
    ot = acc_ref[0] / l_ref[0] - lam * (acc_ref[1] / l_ref[1])
    o = ot.T
    ms = jnp.mean(o * o, axis=-1, keepdims=True)
    o = o * lax.rsqrt(ms + SUBLN_EPS) * sg_ref[...] * (1.0 - lam_init)
    o_ref[0] = (o * szb_ref[0]).astype(BF16)


def _attn(l, lam_vecs, sg, q, k, v, szb):
    b, s, _ = q.shape
    lam_init = 0.8 - 0.6 * math.exp(-0.3 * l)
    qblk = lambda bi, hi, qi: (bi, qi, hi)
    kvblk = lambda bi, hi, qi: (bi, 0, hi)
    stat = lambda: pltpu.VMEM((2, 1, TQ), F32)
    tile = lambda: pltpu.VMEM((2, TK, TQ), F32)
    return pl.pallas_call(
        functools.partial(_attn_kernel, lam_init=lam_init),
        grid=(b, N_DIFF_HEADS, s // TQ),
        in_specs=[
            pl.BlockSpec((None, 4, DIFF_HEAD_DIM), lambda bi, hi, qi: (l, 0, 0)),
            pl.BlockSpec((None, 1, HEAD_W), lambda bi, hi, qi: (l, 0, 0)),
            pl.BlockSpec((1, TQ, HEAD_W), qblk),
            pl.BlockSpec((1, s, HEAD_W), kvblk),
            pl.BlockSpec((1, s, HEAD_W), kvblk),
            pl.BlockSpec((1, TQ, HEAD_W), qblk),
        ],
        out_specs=pl.BlockSpec((1, TQ, HEAD_W), qblk),
        out_shape=jax.ShapeDtypeStruct((b, s, W_B), BF16),
        scratch_shapes=[pltpu.VMEM((HEAD_W, s), BF16), tile(), tile(), stat(), stat(),
                        pltpu.VMEM((2, HEAD_W, TQ), F32)],
        compiler_params=pltpu.CompilerParams(
            dimension_semantics=("arbitrary", "arbitrary", "arbitrary"),
            vmem_limit_bytes=VMEM_LIMIT_BYTES),
        name=f"attn_l{l}",
    )(lam_vecs, sg, q, k, v, szb)


def _outproj_kernel(x_ref, ob_ref, g1_ref, mac_ref, wbo_ref, wo_ref, fg_ref, o_ref, *, final):
    yb = jnp.dot(ob_ref[0], wbo_ref[...], preferred_element_type=F32)
    merged = (mac_ref[0] + g1_ref[0] * yb).astype(BF16)
    xn = x_ref[0] + jnp.dot(merged, wo_ref[...], preferred_element_type=F32)
    if final:
        ms = jnp.mean(xn * xn, axis=-1, keepdims=True)
        xn = xn * lax.rsqrt(ms + NORM_EPS) * fg_ref[...]
    o_ref[0] = xn


def _outproj(l, x, ob, g1, mac, wbo, wo, fg):
    b, s, _ = x.shape
    tm = TM_OUT
    row = lambda bi, ti: (bi, ti, 0)
    lay3 = lambda bi, ti: (l, 0, 0)
    one = pl.Buffered(1)
    return pl.pallas_call(
        functools.partial(_outproj_kernel, final=(l == DEPTH - 1)),
        grid=(b, s // tm),
        in_specs=[
            pl.BlockSpec((1, tm, D_MODEL), row),
            pl.BlockSpec((1, tm, W_B), row),
            pl.BlockSpec((1, tm, D_MODEL), row),
            pl.BlockSpec((1, tm, D_MODEL), row),
            pl.BlockSpec((None, W_B, D_MODEL), lay3, pipeline_mode=one),
            pl.BlockSpec((None, D_MODEL, D_MODEL), lay3, pipeline_mode=one),
            pl.BlockSpec((1, D_MODEL), lambda bi, ti: (0, 0)),
        ],
        out_specs=pl.BlockSpec((1, tm, D_MODEL), row),
        out_shape=jax.ShapeDtypeStruct((b, s, D_MODEL), F32),
        compiler_params=pltpu.CompilerParams(
            dimension_semantics=("arbitrary", "arbitrary"),
            vmem_limit_bytes=VMEM_LIMIT_BYTES),
        name=f"outproj_l{l}",
    )(x, ob, g1, mac, wbo, wo, fg)


def kernel(x, norm_g, w_in, gate_b, conv_a_w, conv_a_b, ln_a_g, ln_a_b, w_a_out, lam_q1, lam_k1,
           lam_q2, lam_k2, subln_g, w_b_out, conv_c_w, w_c_out, w_o, final_g):
    assert x.shape[1] % TM_IN == 0 and x.shape[1] % TM_OUT == 0 and x.shape[1] % TQ == 0
    assert TQ == 2 * TK and TM_IN % CONV_ROWS == 0
    w_in_b = w_in.astype(BF16)
    w_a_out_b = w_a_out.astype(BF16)
    w_b_out_b = w_b_out.astype(BF16)
    w_c_out_b = w_c_out.astype(BF16)
    w_o_b = w_o.astype(BF16)
    lam_vecs = jnp.stack([lam_q1, lam_k1, lam_q2, lam_k2], axis=1)
    fg = final_g.reshape(1, D_MODEL)
    per_layer = lambda a: a.reshape(DEPTH, 1, a.shape[-1])
    ng, gb, cab, lag, lab, sg = map(per_layer, (norm_g, gate_b, conv_a_b, ln_a_g, ln_a_b, subln_g))
    for l in range(DEPTH):
        q, k, v, szb, g1, mac = _inproj(l, x, ng, w_in_b, gb, conv_a_w, cab, lag, lab,
                                        conv_c_w, w_a_out_b, w_c_out_b)
        ob = _attn(l, lam_vecs, sg, q, k, v, szb)
        x = _outproj(l, x, ob, g1, mac, w_b_out_b, w_o_b, fg)
    return x
```

```python
import functools
import math

import jax
import jax.numpy as jnp
from jax import lax
from jax.experimental import pallas as pl
from jax.experimental.pallas import tpu as pltpu

D_MODEL = 1024
DEPTH = 4
W_A = 512
CONV_A_WIDTH = 31
DIFF_HEAD_DIM = 64
N_DIFF_HEADS = 4
HEAD_W = 2 * DIFF_HEAD_DIM
W_B = N_DIFF_HEADS * HEAD_W
W_C = 512
CONV_C_WIDTH = 3
NORM_EPS = 1e-6
SUBLN_EPS = 1e-5
LOG2E = math.log2(math.e)
N_IN = 3 * W_A + 4 * W_B + 4 * W_C + 3 * D_MODEL

A_VAL, A_GATE, A_Z = 0, W_A, 2 * W_A
B_Q = 3 * W_A
B_K, B_V, B_Z = B_Q + W_B, B_Q + 2 * W_B, B_Q + 3 * W_B
C_B = B_Q + 4 * W_B
C_C, C_X, C_Z = C_B + W_C, C_B + 2 * W_C, C_B + 3 * W_C
G_0 = C_B + 4 * W_C
G_1, G_2 = G_0 + D_MODEL, G_0 + 2 * D_MODEL

LANES = 128
SUBLANES = 8
VMEM_LIMIT_BYTES = 56 * 1024 * 1024

TM_IN = 256
TM_OUT = 512
TQ = 1024
QB = 256
TK = 256
SUM_ROWS = 16
CONV_ROWS = 64
HALO_A = 32
HALO_C = 8

F32 = jnp.float32
BF16 = jnp.bfloat16


def _sigmoid(x):
    return jax.nn.sigmoid(x)


def _silu(x):
    return x * jax.nn.sigmoid(x)


def _inproj_kernel(x_ref, ng_ref, wa_ref, wb_ref, wc_ref, wg_ref, gb_ref, caw_ref, cab_ref,
                   lag_ref, lab_ref, ccw_ref, wao_ref, wco_ref,
                   qkvz_ref, g1_ref, mac_ref,
                   exta_ref, extc_ref, conv_ref, h_ref, sza_ref, gate_ref):
    tm = x_ref.shape[1]
    n_chunks = wb_ref.shape[0]
    gate_w = wg_ref.shape[2]

    @pl.when(pl.program_id(1) == 0)
    def _():
        exta_ref[0:HALO_A, :] = jnp.zeros((HALO_A, W_A), F32)
        extc_ref[0:HALO_C, :] = jnp.zeros((HALO_C, W_C), F32)

    x = x_ref[0]
    ms = jnp.mean(x * x, axis=-1, keepdims=True)
    h = (x * lax.rsqrt(ms + NORM_EPS) * ng_ref[...]).astype(BF16)
    h_ref[...] = h

    pa = jnp.dot(h, wa_ref[...], preferred_element_type=F32)
    exta_ref[HALO_A:HALO_A + tm, :] = pa[:, 0:W_A] * _sigmoid(pa[:, W_A:2 * W_A])
    sza_ref[...] = _silu(pa[:, 2 * W_A:3 * W_A])

    def conv_a_chunk(r0, c0):
        cols = slice(c0, c0 + LANES)
        z = [None] * SUBLANES
        for o in range(HALO_A - (CONV_A_WIDTH - 1), HALO_A + 1):
            a, bb = divmod(o, SUBLANES)
            rows = CONV_ROWS + (SUBLANES if bb else 0)
            kk = o - (HALO_A - (CONV_A_WIDTH - 1))
            term = caw_ref[kk:kk + 1, cols] * exta_ref[pl.ds(r0 + SUBLANES * a, rows), cols]
            z[bb] = term if z[bb] is None else z[bb] + term
        acc = z[0]
        for bb in range(1, SUBLANES):
            acc = acc + z[bb][bb:bb + CONV_ROWS, :]
        conv_ref[pl.ds(r0, CONV_ROWS), cols] = acc

    def chunk(i, carry):
        r0 = pl.multiple_of(i * CONV_ROWS, CONV_ROWS)
        for c0 in range(0, W_A, LANES):
            conv_a_chunk(r0, c0)
        hv = h_ref[...]
        scale = jnp.where(i == 0, DIFF_HEAD_DIM ** -0.5 * LOG2E, 1.0).astype(F32)
        gate_ref[i] = _sigmoid(jnp.dot(hv, wg_ref[i], preferred_element_type=F32) + gb_ref[i])
        qkvz_ref[i, 0] = (jnp.dot(hv, wb_ref[i], preferred_element_type=F32) * scale).astype(BF16)
        return carry

    lax.fori_loop(0, n_chunks, chunk, 0)
    exta_ref[0:HALO_A, :] = exta_ref[tm:tm + HALO_A, :]

    pc = jnp.dot(h_ref[...], wc_ref[...], preferred_element_type=F32)
    extc_ref[HALO_C:HALO_C + tm, :] = pc[:, W_C:2 * W_C] * pc[:, 2 * W_C:3 * W_C]
    cbase = HALO_C - (CONV_C_WIDTH - 1)
    cc = ccw_ref[0:1, :] * extc_ref[cbase:cbase + tm, :]
    for kk in range(1, CONV_C_WIDTH):
        cc = cc + ccw_ref[kk:kk + 1, :] * extc_ref[cbase + kk:cbase + kk + tm, :]
    extc_ref[0:HALO_C, :] = extc_ref[tm:tm + HALO_C, :]
    zc = (pc[:, 0:W_C] * cc * _silu(pc[:, 3 * W_C:4 * W_C])).astype(BF16)
    yc = jnp.dot(zc, wco_ref[...], preferred_element_type=F32)

    ca = conv_ref[...] + cab_ref[...]
    mu = jnp.mean(ca, axis=-1, keepdims=True)
    cen = ca - mu
    var = jnp.mean(cen * cen, axis=-1, keepdims=True)
    ua = _silu(cen * lax.rsqrt(var + NORM_EPS) * lag_ref[...] + lab_ref[...])
    za = (ua * sza_ref[...]).astype(BF16)
    ya = jnp.dot(za, wao_ref[...], preferred_element_type=F32)

    def gate(branch):
        parts, lo, hi = [], branch * D_MODEL, (branch + 1) * D_MODEL
        for ci in range(n_chunks):
            s0, s1 = max(lo, ci * gate_w), min(hi, (ci + 1) * gate_w)
            if s0 < s1:
                parts.append(gate_ref[ci, :, s0 - ci * gate_w:s1 - ci * gate_w])
        return jnp.concatenate(parts, axis=-1)

    g1_ref[0] = gate(1)
    mac_ref[0] = gate(0) * ya + gate(2) * yc


def _inproj(l, x, ng, wa, wb, wc, wg, gb, caw, cab, lag, lab, ccw, wao, wco):
    b, s, _ = x.shape
    tm = TM_IN
    n_chunks = tm // CONV_ROWS
    row = lambda bi, ti: (bi, ti, 0)
    lay3 = lambda bi, ti: (l, 0, 0)
    lay4 = lambda bi, ti: (l, 0, 0, 0)
    vec = lambda w: pl.BlockSpec((None, 1, w), lay3)
    one = pl.Buffered(1)
    return pl.pallas_call(
        _inproj_kernel,
        grid=(b, s // tm),
        in_specs=[
            pl.BlockSpec((1, tm, D_MODEL), row),
            vec(D_MODEL),
            pl.BlockSpec((None, D_MODEL, 3 * W_A), lay3, pipeline_mode=one),
            pl.BlockSpec((None, n_chunks, D_MODEL, 4 * W_B // n_chunks), lay4, pipeline_mode=one),
            pl.BlockSpec((None, D_MODEL, 4 * W_C), lay3, pipeline_mode=one),
            pl.BlockSpec((None, n_chunks, D_MODEL, 3 * D_MODEL // n_chunks), lay4, pipeline_mode=one),
            pl.BlockSpec((None, n_chunks, 1, 3 * D_MODEL // n_chunks), lay4),
            pl.BlockSpec((None, CONV_A_WIDTH, W_A), lay3),
            vec(W_A), vec(W_A), vec(W_A),
            pl.BlockSpec((None, CONV_C_WIDTH, W_C), lay3),
            pl.BlockSpec((None, W_A, D_MODEL), lay3, pipeline_mode=one),
            pl.BlockSpec((None, W_C, D_MODEL), lay3, pipeline_mode=one),
        ],
        out_specs=[
            pl.BlockSpec((n_chunks, 1, tm, 4 * W_B // n_chunks), lambda bi, ti: (0, bi, ti, 0)),
            pl.BlockSpec((1, tm, D_MODEL), row),
            pl.BlockSpec((1, tm, D_MODEL), row),
        ],
        out_shape=[
            jax.ShapeDtypeStruct((n_chunks, b, s, 4 * W_B // n_chunks), BF16),
            jax.ShapeDtypeStruct((b, s, D_MODEL), F32),
            jax.ShapeDtypeStruct((b, s, D_MODEL), F32),
        ],
        scratch_shapes=[
            pltpu.VMEM((HALO_A + tm, W_A), F32),
            pltpu.VMEM((HALO_C + tm, W_C), F32),
            pltpu.VMEM((tm, W_A), F32),
            pltpu.VMEM((tm, D_MODEL), BF16),
            pltpu.VMEM((tm, W_A), F32),
            pltpu.VMEM((n_chunks, tm, 3 * D_MODEL // n_chunks), F32),
        ],
        compiler_params=pltpu.CompilerParams(
            dimension_semantics=("arbitrary", "arbitrary"),
            vmem_limit_bytes=VMEM_LIMIT_BYTES),
        name=f"inproj_l{l}",
    )(x, ng, wa, wb, wc, wg, gb, caw, cab, lag, lab, ccw, wao, wco)


def _attn_kernel(lam_ref, sg_ref, q_ref, k_ref, v_ref, bz_ref, o_ref,
                 vt_ref, sa_ref, sb_ref, m_ref, acc_ref, *, lam_init):
    tq = q_ref.shape[1]
    s_len = k_ref.shape[1]
    nqb = tq // QB
    qi = pl.program_id(2)
    unit = lambda mp, c: mp * nqb + c

    @pl.when(qi == 0)
    def _():
        for r0 in range(0, s_len, tq):
            vt_ref[0:HEAD_W, r0:r0 + tq] = v_ref[0, r0:r0 + tq, :].astype(F32).T.astype(BF16)
        vt_ref[HEAD_W:, :] = jnp.ones((SUM_ROWS, s_len), BF16)

    lane = lax.broadcasted_iota(jnp.int32, (1, HEAD_W), 1)
    q_units = {}
    for c in range(nqb):
        q = q_ref[0, c * QB:(c + 1) * QB, :]
        zero = jnp.zeros_like(q)
        q_units[unit(0, c)] = jnp.where(lane < DIFF_HEAD_DIM, q, zero)
        q_units[unit(1, c)] = jnp.where(lane >= DIFF_HEAD_DIM, q, zero)
    contract_last = (((1,), (1,)), ((), ()))

    m_ref[...] = jnp.full(m_ref.shape, -jnp.inf, F32)
    acc_ref[...] = jnp.zeros(acc_ref.shape, F32)

    def scores(key_off, dst_ref, first_block=0):
        kt = k_ref[0, pl.ds(key_off, TK), :]
        for mp in range(2):
            for c in range(first_block, nqb):
                u = unit(mp, c)
                dst_ref[u] = lax.dot_general(kt, q_units[u], contract_last,
                                             preferred_element_type=F32)

    def consume(key_off, src_ref, diag_block=None):
        vt = vt_ref[:, pl.ds(key_off, TK)]
        first_block = 0 if diag_block is None else diag_block
        for mp in range(2):
            for c in range(first_block, nqb):
                u = unit(mp, c)
                st = src_ref[u]
                if c == diag_block:
                    kpos = lax.broadcasted_iota(jnp.int32, (TK, QB), 0)
                    qpos = lax.broadcasted_iota(jnp.int32, (TK, QB), 1)
                    st = jnp.where(kpos <= qpos, st, -jnp.inf)
                m_old = m_ref[u]
                m_new = jnp.maximum(m_old, jnp.max(st, axis=0, keepdims=True))
                alpha = jnp.exp2(m_old - m_new)
                pt = jnp.exp2(st - m_new)
                m_ref[u] = m_new
                acc_ref[u] = alpha * acc_ref[u] + jnp.dot(
                    vt, pt.astype(BF16), preferred_element_type=F32)

    n_full = qi * (tq // TK)
    scores(0, sa_ref)

    def group(i, carry):
        base = pl.multiple_of(i * tq, tq)
        for j in range(0, tq // TK, 2):
            scores(base + (j + 1) * TK, sb_ref)
            consume(base + j * TK, sa_ref)
            scores(base + (j + 2) * TK, sa_ref)
            consume(base + (j + 1) * TK, sb_ref)
        return carry

    lax.fori_loop(0, qi, group, 0)

    doff = pl.multiple_of(qi * tq, tq)
    bufs = (sa_ref, sb_ref)
    for d in range(nqb):
        if d + 1 < nqb:
            scores(doff + (d + 1) * TK, bufs[(d + 1) % 2], first_block=d + 1)
        consume(doff + d * TK, bufs[d % 2], diag_block=d)

    lv = lam_ref[...]
    lam = (jnp.exp(jnp.sum(lv[0:1, :] * lv[1:2, :], axis=-1, keepdims=True))
           - jnp.exp(jnp.sum(lv[2:3, :] * lv[3:4, :], axis=-1, keepdims=True)) + lam_init)
    for c in range(nqb):
        u1, u2 = unit(0, c), unit(1, c)
        a1, a2 = acc_ref[u1], acc_ref[u2]
        ot = (a1[0:HEAD_W] / a1[HEAD_W:HEAD_W + 1]
              - lam * (a2[0:HEAD_W] / a2[HEAD_W:HEAD_W + 1]))
        o = ot.T
        ms = jnp.mean(o * o, axis=-1, keepdims=True)
        o = o * lax.rsqrt(ms + SUBLN_EPS) * sg_ref[...] * (1.0 - lam_init)
        rows = slice(c * QB, (c + 1) * QB)
        o_ref[0, rows, :] = (o * _silu(bz_ref[0, rows, :].astype(F32))).astype(BF16)


def _attn(l, lam_vecs, sg, qkvz):
    _, b, s, _ = qkvz.shape
    lam_init = 0.8 - 0.6 * math.exp(-0.3 * l)
    units = 2 * (TQ // QB)
    stat = lambda: pltpu.VMEM((units, 1, QB), F32)
    tile = lambda: pltpu.VMEM((units, TK, QB), F32)
    qrows = lambda j: pl.BlockSpec((None, 1, TQ, HEAD_W), lambda bi, hi, qi: (j, bi, qi, hi))
    allrows = lambda j: pl.BlockSpec((None, 1, s, HEAD_W), lambda bi, hi, qi: (j, bi, 0, hi))
    return pl.pallas_call(
        functools.partial(_attn_kernel, lam_init=lam_init),
        grid=(b, N_DIFF_HEADS, s // TQ),
        in_specs=[
            pl.BlockSpec((None, 4, DIFF_HEAD_DIM), lambda bi, hi, qi: (l, 0, 0)),
            pl.BlockSpec((None, 1, HEAD_W), lambda bi, hi, qi: (l, 0, 0)),
            qrows(0), allrows(1), allrows(2), qrows(3),
        ],
        out_specs=pl.BlockSpec((1, TQ, HEAD_W), lambda bi, hi, qi: (bi, qi, hi)),
        out_shape=jax.ShapeDtypeStruct((b, s, W_B), BF16),
        scratch_shapes=[pltpu.VMEM((HEAD_W + SUM_ROWS, s), BF16), tile(), tile(), stat(),
                        pltpu.VMEM((units, HEAD_W + SUM_ROWS, QB), F32)],
        compiler_params=pltpu.CompilerParams(
            dimension_semantics=("arbitrary", "arbitrary", "arbitrary"),
            vmem_limit_bytes=VMEM_LIMIT_BYTES),
        name=f"attn_l{l}",
    )(lam_vecs, sg, qkvz, qkvz, qkvz, qkvz)


def _outproj_kernel(x_ref, ob_ref, g1_ref, mac_ref, wbo_ref, wo_ref, fg_ref, o_ref, *, final):
    yb = jnp.dot(ob_ref[0], wbo_ref[...], preferred_element_type=F32)
    merged = (mac_ref[0] + g1_ref[0] * yb).astype(BF16)
    xn = x_ref[0] + jnp.dot(merged, wo_ref[...], preferred_element_type=F32)
    if final:
        ms = jnp.mean(xn * xn, axis=-1, keepdims=True)
        xn = xn * lax.rsqrt(ms + NORM_EPS) * fg_ref[...]
    o_ref[0] = xn


def _outproj(l, x, ob, g1, mac, wbo, wo, fg):
    b, s, _ = x.shape
    tm = TM_OUT
    row = lambda bi, ti: (bi, ti, 0)
    lay3 = lambda bi, ti: (l, 0, 0)
    one = pl.Buffered(1)
    return pl.pallas_call(
        functools.partial(_outproj_kernel, final=(l == DEPTH - 1)),
        grid=(b, s // tm),
        in_specs=[
            pl.BlockSpec((1, tm, D_MODEL), row),
            pl.BlockSpec((1, tm, W_B), row),
            pl.BlockSpec((1, tm, D_MODEL), row),
            pl.BlockSpec((1, tm, D_MODEL), row),
            pl.BlockSpec((None, W_B, D_MODEL), lay3, pipeline_mode=one),
            pl.BlockSpec((None, D_MODEL, D_MODEL), lay3, pipeline_mode=one),
            pl.BlockSpec((1, D_MODEL), lambda bi, ti: (0, 0)),
        ],
        out_specs=pl.BlockSpec((1, tm, D_MODEL), row),
        out_shape=jax.ShapeDtypeStruct((b, s, D_MODEL), F32),
        compiler_params=pltpu.CompilerParams(
            dimension_semantics=("arbitrary", "arbitrary"),
            vmem_limit_bytes=VMEM_LIMIT_BYTES),
        name=f"outproj_l{l}",
    )(x, ob, g1, mac, wbo, wo, fg)


def kernel(x, norm_g, w_in, gate_b, conv_a_w, conv_a_b, ln_a_g, ln_a_b, w_a_out, lam_q1, lam_k1,
           lam_q2, lam_k2, subln_g, w_b_out, conv_c_w, w_c_out, w_o, final_g):
    assert x.shape[1] % TM_IN == 0 and x.shape[1] % TM_OUT == 0 and x.shape[1] % TQ == 0
    assert TK == QB and TQ % (2 * TK) == 0
    n_chunks = TM_IN // CONV_ROWS
    assert (4 * W_B) % (n_chunks * LANES) == 0 and (3 * D_MODEL) % (n_chunks * LANES) == 0
    assert 4 * W_B // n_chunks == W_B
    w_in_b = w_in.astype(BF16)
    chunked = lambda w: w.reshape(DEPTH, D_MODEL, n_chunks, -1).transpose(0, 2, 1, 3)
    w_a = w_in_b[:, :, A_VAL:B_Q]
    w_b = chunked(w_in_b[:, :, B_Q:C_B])
    w_c = w_in_b[:, :, C_B:G_0]
    w_g = chunked(w_in_b[:, :, G_0:N_IN])
    gb = gate_b.reshape(DEPTH, n_chunks, 1, -1)
    w_a_out_b = w_a_out.astype(BF16)
    w_b_out_b = w_b_out.astype(BF16)
    w_c_out_b = w_c_out.astype(BF16)
    w_o_b = w_o.astype(BF16)
    lam_vecs = jnp.stack([lam_q1, lam_k1, lam_q2, lam_k2], axis=1)
    fg = final_g.reshape(1, D_MODEL)
    per_layer = lambda a: a.reshape(DEPTH, 1, a.shape[-1])
    ng, cab, lag, lab, sg = map(per_layer, (norm_g, conv_a_b, ln_a_g, ln_a_b, subln_g))
    for l in range(DEPTH):
        qkvz, g1, mac = _inproj(l, x, ng, w_a, w_b, w_c, w_g, gb, conv_a_w, cab, lag, lab,
                                conv_c_w, w_a_out_b, w_c_out_b)
        ob = _attn(l, lam_vecs, sg, qkvz)
        x = _outproj(l, x, ob, g1, mac, w_b_out_b, w_o_b, fg)
    return x
```

```python
import functools
import math

import jax
import jax.numpy as jnp
from jax import lax
from jax.experimental import pallas as pl
from jax.experimental.pallas import tpu as pltpu

D_MODEL = 1024
DEPTH = 4
W_A = 512
CONV_A_WIDTH = 31
DIFF_HEAD_DIM = 64
N_DIFF_HEADS = 4
HEAD_W = 2 * DIFF_HEAD_DIM
W_B = N_DIFF_HEADS * HEAD_W
W_C = 512
CONV_C_WIDTH = 3
NORM_EPS = 1e-6
SUBLN_EPS = 1e-5
LOG2E = math.log2(math.e)
N_IN = 3 * W_A + 4 * W_B + 4 * W_C + 3 * D_MODEL

A_VAL, A_GATE, A_Z = 0, W_A, 2 * W_A
B_Q = 3 * W_A
B_K, B_V, B_Z = B_Q + W_B, B_Q + 2 * W_B, B_Q + 3 * W_B
C_B = B_Q + 4 * W_B
C_C, C_X, C_Z = C_B + W_C, C_B + 2 * W_C, C_B + 3 * W_C
G_0 = C_B + 4 * W_C
G_1, G_2 = G_0 + D_MODEL, G_0 + 2 * D_MODEL

LANES = 128
SUBLANES = 8
VMEM_LIMIT_BYTES = 56 * 1024 * 1024

TM_IN = 512
N_CHUNKS = 4
TM_OUT = 512
TQ = 1024
QB = 256
TK = 256
SUM_ROWS = 16
CONV_ROWS = 32
HALO_A = 32
HALO_C = 8

F32 = jnp.float32
BF16 = jnp.bfloat16


def _sigmoid(x):
    return jax.nn.sigmoid(x)


def _silu(x):
    return x * jax.nn.sigmoid(x)


def _inproj_kernel(x_ref, ng_ref, wa_ref, wb_ref, wc_ref, wg_ref, gb_ref, caw_ref, cab_ref,
                   lag_ref, lab_ref, ccw_ref, wao_ref, wco_ref,
                   qkvz_ref, g1_ref, mac_ref,
                   exta_ref, extc_ref, conv_ref, h_ref, sza_ref, gate_ref):
    tm = x_ref.shape[1]
    n_chunks = wb_ref.shape[0]
    gate_w = wg_ref.shape[2]

    @pl.when(pl.program_id(1) == 0)
    def _():
        exta_ref[0:HALO_A, :] = jnp.zeros((HALO_A, W_A), F32)
        extc_ref[0:HALO_C, :] = jnp.zeros((HALO_C, W_C), F32)

    x = x_ref[0]
    ms = jnp.mean(x * x, axis=-1, keepdims=True)
    h = (x * lax.rsqrt(ms + NORM_EPS) * ng_ref[...]).astype(BF16)
    h_ref[...] = h

    pa = jnp.dot(h, wa_ref[...], preferred_element_type=F32)
    exta_ref[HALO_A:HALO_A + tm, :] = pa[:, 0:W_A] * _sigmoid(pa[:, W_A:2 * W_A])
    sza_ref[...] = _silu(pa[:, 2 * W_A:3 * W_A])

    def conv_a_chunk(r0, c0):
        cols = slice(c0, c0 + LANES)
        z = [None] * SUBLANES
        for o in range(HALO_A - (CONV_A_WIDTH - 1), HALO_A + 1):
            a, bb = divmod(o, SUBLANES)
            rows = CONV_ROWS + (SUBLANES if bb else 0)
            kk = o - (HALO_A - (CONV_A_WIDTH - 1))
            term = caw_ref[kk:kk + 1, cols] * exta_ref[pl.ds(r0 + SUBLANES * a, rows), cols]
            z[bb] = term if z[bb] is None else z[bb] + term
        acc = z[0]
        for bb in range(1, SUBLANES):
            acc = acc + z[bb][bb:bb + CONV_ROWS, :]
        conv_ref[pl.ds(r0, CONV_ROWS), cols] = acc

    def chunk(i, carry):
        trip_rows = tm // n_chunks
        r0 = pl.multiple_of(i * trip_rows, trip_rows)
        for sub in range(0, trip_rows, CONV_ROWS):
            for c0 in range(0, W_A, LANES):
                conv_a_chunk(r0 + sub, c0)
        hv = h_ref[...]
        scale = jnp.where(i == 0, DIFF_HEAD_DIM ** -0.5 * LOG2E, 1.0).astype(F32)
        gate_ref[i] = _sigmoid(jnp.dot(hv, wg_ref[i], preferred_element_type=F32) + gb_ref[i])
        qkvz_ref[i, 0] = (jnp.dot(hv, wb_ref[i], preferred_element_type=F32) * scale).astype(BF16)
        return carry

    lax.fori_loop(0, n_chunks, chunk, 0)
    exta_ref[0:HALO_A, :] = exta_ref[tm:tm + HALO_A, :]

    pc = jnp.dot(h_ref[...], wc_ref[...], preferred_element_type=F32)
    extc_ref[HALO_C:HALO_C + tm, :] = pc[:, W_C:2 * W_C] * pc[:, 2 * W_C:3 * W_C]
    cbase = HALO_C - (CONV_C_WIDTH - 1)
    cc = ccw_ref[0:1, :] * extc_ref[cbase:cbase + tm, :]
    for kk in range(1, CONV_C_WIDTH):
        cc = cc + ccw_ref[kk:kk + 1, :] * extc_ref[cbase + kk:cbase + kk + tm, :]
    extc_ref[0:HALO_C, :] = extc_ref[tm:tm + HALO_C, :]
    zc = (pc[:, 0:W_C] * cc * _silu(pc[:, 3 * W_C:4 * W_C])).astype(BF16)
    yc = jnp.dot(zc, wco_ref[...], preferred_element_type=F32)

    ca = conv_ref[...] + cab_ref[...]
    mu = jnp.mean(ca, axis=-1, keepdims=True)
    cen = ca - mu
    var = jnp.mean(cen * cen, axis=-1, keepdims=True)
    ua = _silu(cen * lax.rsqrt(var + NORM_EPS) * lag_ref[...] + lab_ref[...])
    za = (ua * sza_ref[...]).astype(BF16)
    ya = jnp.dot(za, wao_ref[...], preferred_element_type=F32)

    def gate(branch):
        parts, lo, hi = [], branch * D_MODEL, (branch + 1) * D_MODEL
        for ci in range(n_chunks):
            s0, s1 = max(lo, ci * gate_w), min(hi, (ci + 1) * gate_w)
            if s0 < s1:
                parts.append(gate_ref[ci, :, s0 - ci * gate_w:s1 - ci * gate_w])
        return jnp.concatenate(parts, axis=-1)

    g1_ref[0] = gate(1).astype(BF16)
    mac_ref[0] = (gate(0) * ya + gate(2) * yc).astype(BF16)


def _inproj(l, x, ng, wa, wb, wc, wg, gb, caw, cab, lag, lab, ccw, wao, wco):
    b, s, _ = x.shape
    tm = TM_IN
    n_chunks = N_CHUNKS
    row = lambda bi, ti: (bi, ti, 0)
    lay3 = lambda bi, ti: (l, 0, 0)
    lay4 = lambda bi, ti: (l, 0, 0, 0)
    vec = lambda w: pl.BlockSpec((None, 1, w), lay3)
    one = pl.Buffered(1)
    return pl.pallas_call(
        _inproj_kernel,
        grid=(b, s // tm),
        in_specs=[
            pl.BlockSpec((1, tm, D_MODEL), row),
            vec(D_MODEL),
            pl.BlockSpec((None, D_MODEL, 3 * W_A), lay3, pipeline_mode=one),
            pl.BlockSpec((None, n_chunks, D_MODEL, 4 * W_B // n_chunks), lay4, pipeline_mode=one),
            pl.BlockSpec((None, D_MODEL, 4 * W_C), lay3, pipeline_mode=one),
            pl.BlockSpec((None, n_chunks, D_MODEL, 3 * D_MODEL // n_chunks), lay4, pipeline_mode=one),
            pl.BlockSpec((None, n_chunks, 1, 3 * D_MODEL // n_chunks), lay4),
            pl.BlockSpec((None, CONV_A_WIDTH, W_A), lay3),
            vec(W_A), vec(W_A), vec(W_A),
            pl.BlockSpec((None, CONV_C_WIDTH, W_C), lay3),
            pl.BlockSpec((None, W_A, D_MODEL), lay3, pipeline_mode=one),
            pl.BlockSpec((None, W_C, D_MODEL), lay3, pipeline_mode=one),
        ],
        out_specs=[
            pl.BlockSpec((n_chunks, 1, tm, 4 * W_B // n_chunks), lambda bi, ti: (0, bi, ti, 0)),
            pl.BlockSpec((1, tm, D_MODEL), row),
            pl.BlockSpec((1, tm, D_MODEL), row),
        ],
        out_shape=[
            jax.ShapeDtypeStruct((n_chunks, b, s, 4 * W_B // n_chunks), BF16),
            jax.ShapeDtypeStruct((b, s, D_MODEL), BF16),
            jax.ShapeDtypeStruct((b, s, D_MODEL), BF16),
        ],
        scratch_shapes=[
            pltpu.VMEM((HALO_A + tm, W_A), F32),
            pltpu.VMEM((HALO_C + tm, W_C), F32),
            pltpu.VMEM((tm, W_A), F32),
            pltpu.VMEM((tm, D_MODEL), BF16),
            pltpu.VMEM((tm, W_A), F32),
            pltpu.VMEM((n_chunks, tm, 3 * D_MODEL // n_chunks), F32),
        ],
        compiler_params=pltpu.CompilerParams(
            dimension_semantics=("arbitrary", "arbitrary"),
            vmem_limit_bytes=VMEM_LIMIT_BYTES),
        name=f"inproj_l{l}",
    )(x, ng, wa, wb, wc, wg, gb, caw, cab, lag, lab, ccw, wao, wco)


def _attn_kernel(lam_ref, sg_ref, q_ref, k_ref, v_ref, bz_ref, o_ref,
                 vt_ref, sa_ref, sb_ref, m_ref, acc_ref, *, lam_init):
    tq = q_ref.shape[1]
    s_len = k_ref.shape[1]
    nqb = tq // QB
    qi = pl.program_id(2)
    unit = lambda mp, c: mp * nqb + c

    @pl.when(qi == 0)
    def _():
        for r0 in range(0, s_len, tq):
            vt_ref[0:HEAD_W, r0:r0 + tq] = v_ref[0, r0:r0 + tq, :].astype(F32).T.astype(BF16)
        vt_ref[HEAD_W:, :] = jnp.ones((SUM_ROWS, s_len), BF16)

    lane = lax.broadcasted_iota(jnp.int32, (1, HEAD_W), 1)
    q_units = {}
    for c in range(nqb):
        q = q_ref[0, c * QB:(c + 1) * QB, :]
        zero = jnp.zeros_like(q)
        q_units[unit(0, c)] = jnp.where(lane < DIFF_HEAD_DIM, q, zero)
        q_units[unit(1, c)] = jnp.where(lane >= DIFF_HEAD_DIM, q, zero)
    contract_last = (((1,), (1,)), ((), ()))

    m_ref[...] = jnp.full(m_ref.shape, -jnp.inf, F32)
    acc_ref[...] = jnp.zeros(acc_ref.shape, F32)

    def scores(key_off, dst_ref, first_block=0):
        kt = k_ref[0, pl.ds(key_off, TK), :]
        for mp in range(2):
            for c in range(first_block, nqb):
                u = unit(mp, c)
                dst_ref[u] = lax.dot_general(kt, q_units[u], contract_last,
                                             preferred_element_type=F32)

    def consume(key_off, src_ref, diag_block=None):
        vt = vt_ref[:, pl.ds(key_off, TK)]
        first_block = 0 if diag_block is None else diag_block
        for mp in range(2):
            for c in range(first_block, nqb):
                u = unit(mp, c)
                st = src_ref[u]
                if c == diag_block:
                    kpos = lax.broadcasted_iota(jnp.int32, (TK, QB), 0)
                    qpos = lax.broadcasted_iota(jnp.int32, (TK, QB), 1)
                    st = jnp.where(kpos <= qpos, st, -jnp.inf)
                m_old = m_ref[u]
                m_new = jnp.maximum(m_old, jnp.max(st, axis=0, keepdims=True))
                alpha = jnp.exp2(m_old - m_new)
                pt = jnp.exp2(st - m_new)
                m_ref[u] = m_new
                acc_ref[u] = alpha * acc_ref[u] + jnp.dot(
                    vt, pt.astype(BF16), preferred_element_type=F32)

    scores(0, sa_ref)

    def group(i, carry):
        base = pl.multiple_of(i * tq, tq)
        for j in range(0, tq // TK, 2):
            scores(base + (j + 1) * TK, sb_ref)
            consume(base + j * TK, sa_ref)
            scores(base + (j + 2) * TK, sa_ref)
            consume(base + (j + 1) * TK, sb_ref)
        return carry

    lax.fori_loop(0, qi, group, 0)

    doff = pl.multiple_of(qi * tq, tq)
    bufs = (sa_ref, sb_ref)
    for d in range(nqb):
        if d + 1 < nqb:
            scores(doff + (d + 1) * TK, bufs[(d + 1) % 2], first_block=d + 1)
        consume(doff + d * TK, bufs[d % 2], diag_block=d)

    lv = lam_ref[...]
    lam = (jnp.exp(jnp.sum(lv[0:1, :] * lv[1:2, :], axis=-1, keepdims=True))
           - jnp.exp(jnp.sum(lv[2:3, :] * lv[3:4, :], axis=-1, keepdims=True)) + lam_init)
    for c in range(nqb):
        u1, u2 = unit(0, c), unit(1, c)
        a1, a2 = acc_ref[u1], acc_ref[u2]
        ot = (a1[0:HEAD_W] / a1[HEAD_W:HEAD_W + 1]
              - lam * (a2[0:HEAD_W] / a2[HEAD_W:HEAD_W + 1]))
        o = ot.T
        ms = jnp.mean(o * o, axis=-1, keepdims=True)
        o = o * lax.rsqrt(ms + SUBLN_EPS) * sg_ref[...] * (1.0 - lam_init)
        rows = slice(c * QB, (c + 1) * QB)
        o_ref[0, rows, :] = (o * _silu(bz_ref[0, rows, :].astype(F32))).astype(BF16)


def _attn(l, lam_vecs, sg, qkvz):
    _, b, s, _ = qkvz.shape
    lam_init = 0.8 - 0.6 * math.exp(-0.3 * l)
    units = 2 * (TQ // QB)
    stat = lambda: pltpu.VMEM((units, 1, QB), F32)
    tile = lambda: pltpu.VMEM((units, TK, QB), F32)
    qrows = lambda j: pl.BlockSpec((None, 1, TQ, HEAD_W), lambda bi, hi, qi: (j, bi, qi, hi))
    allrows = lambda j: pl.BlockSpec((None, 1, s, HEAD_W), lambda bi, hi, qi: (j, bi, 0, hi))
    return pl.pallas_call(
        functools.partial(_attn_kernel, lam_init=lam_init),
        grid=(b, N_DIFF_HEADS, s // TQ),
        in_specs=[
            pl.BlockSpec((None, 4, DIFF_HEAD_DIM), lambda bi, hi, qi: (l, 0, 0)),
            pl.BlockSpec((None, 1, HEAD_W), lambda bi, hi, qi: (l, 0, 0)),
            qrows(0), allrows(1), allrows(2), qrows(3),
        ],
        out_specs=pl.BlockSpec((1, TQ, HEAD_W), lambda bi, hi, qi: (bi, qi, hi)),
        out_shape=jax.ShapeDtypeStruct((b, s, W_B), BF16),
        scratch_shapes=[pltpu.VMEM((HEAD_W + SUM_ROWS, s), BF16), tile(), tile(), stat(),
                        pltpu.VMEM((units, HEAD_W + SUM_ROWS, QB), F32)],
        compiler_params=pltpu.CompilerParams(
            dimension_semantics=("arbitrary", "arbitrary", "arbitrary"),
            vmem_limit_bytes=VMEM_LIMIT_BYTES),
        name=f"attn_l{l}",
    )(lam_vecs, sg, qkvz, qkvz, qkvz, qkvz)


def _outproj_kernel(x_ref, ob_ref, g1_ref, mac_ref, wbo_ref, wo_ref, fg_ref, o_ref, *, final):
    yb = jnp.dot(ob_ref[0], wbo_ref[...], preferred_element_type=F32)
    merged = (mac_ref[0].astype(F32) + g1_ref[0].astype(F32) * yb).astype(BF16)
    xn = x_ref[0] + jnp.dot(merged, wo_ref[...], preferred_element_type=F32)
    if final:
        ms = jnp.mean(xn * xn, axis=-1, keepdims=True)
        xn = xn * lax.rsqrt(ms + NORM_EPS) * fg_ref[...]
    o_ref[0] = xn


def _outproj(l, x, ob, g1, mac, wbo, wo, fg):
    b, s, _ = x.shape
    tm = TM_OUT
    row = lambda bi, ti: (bi, ti, 0)
    lay3 = lambda bi, ti: (l, 0, 0)
    one = pl.Buffered(1)
    return pl.pallas_call(
        functools.partial(_outproj_kernel, final=(l == DEPTH - 1)),
        grid=(b, s // tm),
        in_specs=[
            pl.BlockSpec((1, tm, D_MODEL), row),
            pl.BlockSpec((1, tm, W_B), row),
            pl.BlockSpec((1, tm, D_MODEL), row),
            pl.BlockSpec((1, tm, D_MODEL), row),
            pl.BlockSpec((None, W_B, D_MODEL), lay3, pipeline_mode=one),
            pl.BlockSpec((None, D_MODEL, D_MODEL), lay3, pipeline_mode=one),
            pl.BlockSpec((1, D_MODEL), lambda bi, ti: (0, 0)),
        ],
        out_specs=pl.BlockSpec((1, tm, D_MODEL), row),
        out_shape=jax.ShapeDtypeStruct((b, s, D_MODEL), F32),
        compiler_params=pltpu.CompilerParams(
            dimension_semantics=("arbitrary", "arbitrary"),
            vmem_limit_bytes=VMEM_LIMIT_BYTES),
        name=f"outproj_l{l}",
    )(x, ob, g1, mac, wbo, wo, fg)


def kernel(x, norm_g, w_in, gate_b, conv_a_w, conv_a_b, ln_a_g, ln_a_b, w_a_out, lam_q1, lam_k1,
           lam_q2, lam_k2, subln_g, w_b_out, conv_c_w, w_c_out, w_o, final_g):
    assert x.shape[1] % TM_IN == 0 and x.shape[1] % TM_OUT == 0 and x.shape[1] % TQ == 0
    assert TK == QB and TQ % (2 * TK) == 0
    n_chunks = N_CHUNKS
    assert TM_IN % (n_chunks * CONV_ROWS) == 0
    assert (4 * W_B) % (n_chunks * LANES) == 0 and (3 * D_MODEL) % (n_chunks * LANES) == 0
    assert 4 * W_B // n_chunks == W_B
    chunked = lambda w: w.reshape(DEPTH, D_MODEL, n_chunks, -1).transpose(0, 2, 1, 3)
    w_a = w_in[:, :, A_VAL:B_Q].astype(BF16)
    w_b = chunked(w_in[:, :, B_Q:C_B]).astype(BF16)
    w_c = w_in[:, :, C_B:G_0].astype(BF16)
    w_g = chunked(w_in[:, :, G_0:N_IN]).astype(BF16)
    gb = gate_b.reshape(DEPTH, n_chunks, 1, -1)
    w_a_out_b = w_a_out.astype(BF16)
    w_b_out_b = w_b_out.astype(BF16)
    w_c_out_b = w_c_out.astype(BF16)
    w_o_b = w_o.astype(BF16)
    lam_vecs = jnp.stack([lam_q1, lam_k1, lam_q2, lam_k2], axis=1)
    fg = final_g.reshape(1, D_MODEL)
    per_layer = lambda a: a.reshape(DEPTH, 1, a.shape[-1])
    ng, cab, lag, lab, sg = map(per_layer, (norm_g, conv_a_b, ln_a_g, ln_a_b, subln_g))
    for l in range(DEPTH):
        qkvz, g1, mac = _inproj(l, x, ng, w_a, w_b, w_c, w_g, gb, conv_a_w, cab, lag, lab,
                                conv_c_w, w_a_out_b, w_c_out_b)
        ob = _attn(l, lam_vecs, sg, qkvz)
        x = _outproj(l, x, ob, g1, mac, w_b_out_b, w_o_b, fg)
    return x
```
